```python
import jax
import jax.numpy as jnp
from jax import lax
import numpy as np

D_MODEL = 1024
BATCH = 4
SEQ = 8192
DEPTH = 1
DEC_BATCH = 32
DEC_SEQ = 4
PAST_LEN = 16384
PAGE_SIZE = 128

ATT_HEADS = 8
ATT_DH = 64
ATT_W = ATT_HEADS * ATT_DH
IDX_HEADS = 8
IDX_DH = 64
IDX_SCALE = (IDX_HEADS * IDX_DH) ** -0.5
TOPK_MAX = 256
Q_BLOCK = 128
LRU_W = D_MODEL - ATT_W
LRU_BLOCKS = 8
LRU_BW = LRU_W // LRU_BLOCKS
LRU_C = 8.0
CONV_W = 4
MIX_W = ATT_W + LRU_W
IN_SIZES = (ATT_W, ATT_W, ATT_W, IDX_HEADS * IDX_DH, IDX_DH, IDX_HEADS, LRU_W, LRU_W)
IN_COLS = sum(IN_SIZES)
IN_SPLITS = tuple(int(c) for c in np.cumsum(IN_SIZES)[:-1])
N_MEM = 256
MEM_HEADS = 4
MEM_DH = D_MODEL // MEM_HEADS
N_GROUPS = 4
EXP_PER_GROUP = 8
N_EXPERTS = N_GROUPS * EXP_PER_GROUP
EXP_FF = D_MODEL // 4
MOE_TOPK = 2
DN_ALPHA = (2.0 * DEPTH) ** 0.25
DN_BETA = (8.0 * DEPTH) ** -0.25
LN_EPS = 1e-5

kernel_name = 'hybrid_dsa_rglru_hmoe_decode_step'


def layer_norm(x, g, b):
    xf = x.astype(jnp.float32)
    mu = jnp.mean(xf, axis=-1, keepdims=True)
    var = jnp.mean(jnp.square(xf - mu), axis=-1, keepdims=True)
    y = (xf - mu) * lax.rsqrt(var + LN_EPS) * g.astype(jnp.float32) + b.astype(jnp.float32)
    return y.astype(x.dtype)


def dsa_core(q, qi, wi, q_pos, kidx, n_top, gather_kv):
    n_keys = kidx.shape[1]
    dots = jnp.einsum('bthd,bsd->bths', qi, kidx).astype(jnp.float32)
    score = jnp.einsum('bths,bth->bts', jax.nn.relu(dots), wi.astype(jnp.float32) * IDX_SCALE)
    admissible = jnp.arange(n_keys)[None, :] <= q_pos[:, None]
    score = jnp.where(admissible[None], score, -jnp.inf)
    _, idx = lax.top_k(score, n_top)
    valid = idx <= q_pos[None, :, None]
    k_sel, v_sel = gather_kv(idx)
    logits = jnp.einsum('bthd,btkhd->bthk', q, k_sel).astype(jnp.float32) * ATT_DH ** -0.5
    logits = jnp.where(valid[:, :, None, :], logits, -jnp.inf)
    p = jax.nn.softmax(logits, axis=-1).astype(v_sel.dtype)
    return jnp.einsum('bthk,btkhd->bthd', p, v_sel)


def dsa_prompt(q, k, v, qi, ki, wi):
    bsz, s_len = q.shape[:2]
    n_top = min(TOPK_MAX, s_len // 4)
    n_blk = s_len // Q_BLOCK
    bidx = jnp.arange(bsz)[:, None, None]

    def gather_kv(idx):
        return k[bidx, idx], v[bidx, idx]

    def to_blocks(a):
        return jnp.swapaxes(a.reshape((bsz, n_blk, Q_BLOCK) + a.shape[2:]), 0, 1)

    def one_block(args):
        qb, qib, wib, pb = args
        return dsa_core(qb, qib, wib, pb, ki, n_top, gather_kv)

    pos = jnp.arange(s_len).reshape(n_blk, Q_BLOCK)
    out = lax.map(one_block, (to_blocks(q), to_blocks(qi), to_blocks(wi), pos))
    return jnp.swapaxes(out, 0, 1).reshape(bsz, s_len, ATT_W)


def make_sample_attend(cache_k, cache_v, cache_kidx, page_table, layer):
    def attend(q, k, v, qi, ki, wi):
        bsz, t_len = q.shape[:2]
        n_keys = PAST_LEN + t_len
        n_top = min(TOPK_MAX, n_keys // 4)
        bidx = jnp.arange(bsz)[:, None, None]
        ki_past = cache_kidx[page_table, layer].reshape(bsz, PAST_LEN, IDX_DH)
        kidx = jnp.concatenate([ki_past.astype(ki.dtype), ki], axis=1)

        def gather_kv(idx):
            in_past = (idx < PAST_LEN)[..., None, None]
            lp = jnp.minimum(idx, PAST_LEN - 1)
            phys = page_table[bidx, lp // PAGE_SIZE]
            off = lp % PAGE_SIZE
            ln = jnp.clip(idx - PAST_LEN, 0, t_len - 1)
            k_sel = jnp.where(in_past, cache_k[phys, layer, off].astype(k.dtype), k[bidx, ln])
            v_sel = jnp.where(in_past, cache_v[phys, layer, off].astype(v.dtype), v[bidx, ln])
            return k_sel, v_sel

        pos = PAST_LEN + jnp.arange(t_len)
        return dsa_core(q, qi, wi, pos, kidx, n_top, gather_kv).reshape(bsz, t_len, ATT_W)
    return attend


def rglru(xl, conv_buf, h0, conv_w, conv_b, w_a, b_a, w_x, b_x, lam):
    bsz, t_len, _ = xl.shape
    xp = jnp.concatenate([conv_buf.astype(xl.dtype), xl], axis=1)
    xc = conv_b + conv_w[CONV_W - 1] * xl
    for j in range(CONV_W - 1):
        xc = xc + conv_w[j] * xp[:, j:j + t_len]
    xb = xc.reshape(bsz, t_len, LRU_BLOCKS, LRU_BW)
    r = jax.nn.sigmoid(jnp.einsum('btnd,nde->btne', xb, w_a).reshape(bsz, t_len, LRU_W) + b_a)
    i = jax.nn.sigmoid(jnp.einsum('btnd,nde->btne', xb, w_x).reshape(bsz, t_len, LRU_W) + b_x)
    log_a = -LRU_C * jax.nn.softplus(-lam.astype(jnp.float32)) * r.astype(jnp.float32)
    a = jnp.exp(log_a)
    u = jnp.sqrt(-jnp.expm1(2.0 * log_a)) * (i * xc).astype(jnp.float32)

    def step(h, au):
        h = au[0] * h + au[1]
        return h, h

    h_last, hs = lax.scan(step, h0.astype(jnp.float32), (jnp.swapaxes(a, 0, 1), jnp.swapaxes(u, 0, 1)))
    return jnp.swapaxes(hs, 0, 1).astype(xl.dtype), h_last.astype(xl.dtype), xp[:, -(CONV_W - 1):]


def mem_kv(mem, w_mk, w_mv):
    bsz, n_mem, _ = mem.shape
    mk = (mem @ w_mk).reshape(bsz, n_mem, MEM_HEADS, MEM_DH)
    mv = (mem @ w_mv).reshape(bsz, n_mem, MEM_HEADS, MEM_DH)
    return mk, mv


def mem_attend(x, mk, mv, w_mq, w_mo):
    bsz, t_len, _ = x.shape
    q = (x @ w_mq).reshape(bsz, t_len, MEM_HEADS, MEM_DH)
    logits = jnp.einsum('bthd,bmhd->bhtm', q, mk.astype(q.dtype)).astype(jnp.float32) * MEM_DH ** -0.5
    p = jax.nn.softmax(logits, axis=-1).astype(x.dtype)
    o = jnp.einsum('bhtm,bmhd->bthd', p, mv.astype(x.dtype)).reshape(bsz, t_len, D_MODEL)
    return o @ w_mo


def hmoe(x, w_rg, b_rg, w_re, b_re, w_gate, w_up, w_down):
    def per_sequence(xs):
        g_logits = (xs @ w_rg).astype(jnp.float32) + b_rg.astype(jnp.float32)
        g_prob = jax.nn.softmax(g_logits, axis=-1)
        _, g_top = lax.top_k(g_logits, 1)
        p_group = jnp.take_along_axis(g_prob, g_top, axis=-1)
        g_onehot = jax.nn.one_hot(g_top[:, 0], N_GROUPS, dtype=jnp.float32)
        e_all = jnp.einsum('td,gde->tge', xs, w_re).astype(jnp.float32) + b_re.astype(jnp.float32)
        e_logits = jnp.einsum('tge,tg->te', e_all, g_onehot)
        e_val, e_top = lax.top_k(e_logits, MOE_TOPK)
        e_w = jax.nn.softmax(e_val, axis=-1) * p_group
        e_id = g_top * EXP_PER_GROUP + e_top
        comb = jnp.einsum('tk,tke->te', e_w, jax.nn.one_hot(e_id, N_EXPERTS, dtype=jnp.float32)).astype(xs.dtype)
        hg = jnp.einsum('td,edf->tef', xs, w_gate)
        hu = jnp.einsum('td,edf->tef', xs, w_up)
        act = jax.nn.silu(hg) * hu * comb[:, :, None]
        return jnp.einsum('tef,efd->td', act, w_down)
    return lax.map(per_sequence, x)


def layer_forward(x, attend, conv_buf, h0, mem_k, mem_v, lw):
    bsz, t_len, _ = x.shape
    q, k, v, qi, ki, wi, xl, gl = jnp.split(x @ lw['w_in'], IN_SPLITS, axis=-1)
    q = q.reshape(bsz, t_len, ATT_HEADS, ATT_DH)
    k = k.reshape(bsz, t_len, ATT_HEADS, ATT_DH)
    v = v.reshape(bsz, t_len, ATT_HEADS, ATT_DH)
    qi = qi.reshape(bsz, t_len, IDX_HEADS, IDX_DH)
    att = attend(q, k, v, qi, ki, wi)
    hs, h_last, conv_new = rglru(xl, conv_buf, h0, lw['conv_w'], lw['conv_b'], lw['w_rg_a'], lw['b_rg_a'],
                                 lw['w_rg_x'], lw['b_rg_x'], lw['lru_lambda'])
    rec = hs * jax.nn.gelu(gl)
    mix = jnp.concatenate([att, rec], axis=-1) @ lw['w_out']
    x = layer_norm(DN_ALPHA * x + mix, lw['ln1_g'], lw['ln1_b'])
    x = layer_norm(DN_ALPHA * x + mem_attend(x, mem_k, mem_v, lw['w_mq'], lw['w_mo']), lw['ln2_g'], lw['ln2_b'])
    ffn = hmoe(x, lw['w_router_g'], lw['b_router_g'], lw['w_router_e'], lw['b_router_e'],
               lw['w_exp_gate'], lw['w_exp_up'], lw['w_exp_down'])
    x = layer_norm(DN_ALPHA * x + ffn, lw['ln3_g'], lw['ln3_b'])
    return x, (k, v, ki), (conv_new, h_last)


def setup_inputs(seed: int = 0) -> dict:
    key = jax.random.key(seed)
    keys = iter(jax.random.split(key, 48))

    def nrm(shape, scale=1.0):
        return jax.random.normal(next(keys), shape, jnp.float32) * scale

    n_pages = PAST_LEN // PAGE_SIZE
    n_used = DEC_BATCH * n_pages
    n_phys = (5 * n_used + 3) // 4
    x_prompt = nrm((BATCH, SEQ, D_MODEL))
    x_sample = nrm((DEC_BATCH, DEC_SEQ, D_MODEL))
    cache_k = nrm((n_phys, DEPTH, PAGE_SIZE, ATT_HEADS, ATT_DH))
    cache_v = nrm((n_phys, DEPTH, PAGE_SIZE, ATT_HEADS, ATT_DH), 0.6)
    cache_kidx = nrm((n_phys, DEPTH, PAGE_SIZE, IDX_DH))
    cache_mem_k = nrm((DEC_BATCH, DEPTH, N_MEM, MEM_HEADS, MEM_DH))
    cache_mem_v = nrm((DEC_BATCH, DEPTH, N_MEM, MEM_HEADS, MEM_DH), 0.6)
    state_conv = nrm((DEC_BATCH, DEPTH, CONV_W - 1, LRU_W))
    state_h = nrm((DEC_BATCH, DEPTH, LRU_W), 0.5)
    page_table = jax.random.permutation(next(keys), n_phys)[:n_used].reshape(DEC_BATCH, n_pages).astype(jnp.int32)
    mem_prompt = nrm((BATCH, N_MEM, D_MODEL))
    col_scale = jnp.ones((IN_COLS,), jnp.float32).at[IN_SPLITS[1]:IN_SPLITS[2]].set(DN_BETA)
    w_in = nrm((DEPTH, D_MODEL, IN_COLS), D_MODEL ** -0.5) * col_scale
    conv_w = nrm((DEPTH, CONV_W, LRU_W), CONV_W ** -0.5)
    conv_b = nrm((DEPTH, LRU_W), 0.02)
    w_rg_a = nrm((DEPTH, LRU_BLOCKS, LRU_BW, LRU_BW), LRU_BW ** -0.5)
    b_rg_a = nrm((DEPTH, LRU_W), 0.02)
    w_rg_x = nrm((DEPTH, LRU_BLOCKS, LRU_BW, LRU_BW), LRU_BW ** -0.5)
    b_rg_x = nrm((DEPTH, LRU_W), 0.02)
    a_pow = jax.random.uniform(next(keys), (DEPTH, LRU_W), jnp.float32, 0.9, 0.999)
    a_base = a_pow ** (1.0 / LRU_C)
    lru_lambda = jnp.log(a_base) - jnp.log1p(-a_base)
    w_out = nrm((DEPTH, MIX_W, D_MODEL), MIX_W ** -0.5 * DN_BETA)
    ln1_g = 1.0 + nrm((DEPTH, D_MODEL), 0.02)
    ln1_b = nrm((DEPTH, D_MODEL), 0.02)
    w_mq = nrm((DEPTH, D_MODEL, D_MODEL), D_MODEL ** -0.5)
    w_mk = nrm((DEPTH, D_MODEL, D_MODEL), D_MODEL ** -0.5)
    w_mv = nrm((DEPTH, D_MODEL, D_MODEL), D_MODEL ** -0.5 * DN_BETA)
    w_mo = nrm((DEPTH, D_MODEL, D_MODEL), D_MODEL ** -0.5 * DN_BETA)
    ln2_g = 1.0 + nrm((DEPTH, D_MODEL), 0.02)
    ln2_b = nrm((DEPTH, D_MODEL), 0.02)
    w_router_g = nrm((DEPTH, D_MODEL, N_GROUPS), D_MODEL ** -0.5)
    b_router_g = nrm((DEPTH, N_GROUPS), 0.01)
    w_router_e = nrm((DEPTH, N_GROUPS, D_MODEL, EXP_PER_GROUP), D_MODEL ** -0.5)
    b_router_e = nrm((DEPTH, N_GROUPS, EXP_PER_GROUP), 0.01)
    w_exp_gate = nrm((DEPTH, N_EXPERTS, D_MODEL, EXP_FF), D_MODEL ** -0.5 * DN_BETA)
    w_exp_up = nrm((DEPTH, N_EXPERTS, D_MODEL, EXP_FF), D_MODEL ** -0.5 * DN_BETA)
    w_exp_down = nrm((DEPTH, N_EXPERTS, EXP_FF, D_MODEL), EXP_FF ** -0.5 * DN_BETA)
    ln3_g = 1.0 + nrm((DEPTH, D_MODEL), 0.02)
    ln3_b = nrm((DEPTH, D_MODEL), 0.02)
    return {'x_prompt': x_prompt, 'x_sample': x_sample, 'cache_k': cache_k, 'cache_v': cache_v,
            'cache_kidx': cache_kidx, 'cache_mem_k': cache_mem_k, 'cache_mem_v': cache_mem_v,
            'state_conv': state_conv, 'state_h': state_h, 'page_table': page_table, 'mem_prompt': mem_prompt,
            'w_in': w_in, 'conv_w': conv_w, 'conv_b': conv_b, 'w_rg_a': w_rg_a, 'b_rg_a': b_rg_a,
            'w_rg_x': w_rg_x, 'b_rg_x': b_rg_x, 'lru_lambda': lru_lambda, 'w_out': w_out,
            'ln1_g': ln1_g, 'ln1_b': ln1_b, 'w_mq': w_mq, 'w_mk': w_mk, 'w_mv': w_mv, 'w_mo': w_mo,
            'ln2_g': ln2_g, 'ln2_b': ln2_b, 'w_router_g': w_router_g, 'b_router_g': b_router_g,
            'w_router_e': w_router_e, 'b_router_e': b_router_e, 'w_exp_gate': w_exp_gate,
            'w_exp_up': w_exp_up, 'w_exp_down': w_exp_down, 'ln3_g': ln3_g, 'ln3_b': ln3_b}


def reference(x_prompt, x_sample, cache_k, cache_v, cache_kidx, cache_mem_k, cache_mem_v, state_conv, state_h,
              page_table, mem_prompt, w_in, conv_w, conv_b, w_rg_a, b_rg_a, w_rg_x, b_rg_x, lru_lambda, w_out,
              ln1_g, ln1_b, w_mq, w_mk, w_mv, w_mo, ln2_g, ln2_b, w_router_g, b_router_g, w_router_e,
              b_router_e, w_exp_gate, w_exp_up, w_exp_down, ln3_g, ln3_b):
    yp, ys = x_prompt, x_sample
    bp = x_prompt.shape[0]
    kp_l, vp_l, kip_l, mkp_l, mvp_l, cp_l, hp_l = [], [], [], [], [], [], []
    ks_l, vs_l, kis_l, cs_l, hs_l = [], [], [], [], []
    for l in range(DEPTH):
        lw = {'w_in': w_in[l], 'conv_w': conv_w[l], 'conv_b': conv_b[l], 'w_rg_a': w_rg_a[l],
              'b_rg_a': b_rg_a[l], 'w_rg_x': w_rg_x[l], 'b_rg_x': b_rg_x[l], 'lru_lambda': lru_lambda[l],
              'w_out': w_out[l], 'ln1_g': ln1_g[l], 'ln1_b': ln1_b[l], 'w_mq': w_mq[l], 'w_mo': w_mo[l],
              'ln2_g': ln2_g[l], 'ln2_b': ln2_b[l], 'w_router_g': w_router_g[l], 'b_router_g': b_router_g[l],
              'w_router_e': w_router_e[l], 'b_router_e': b_router_e[l], 'w_exp_gate': w_exp_gate[l],
              'w_exp_up': w_exp_up[l], 'w_exp_down': w_exp_down[l], 'ln3_g': ln3_g[l], 'ln3_b': ln3_b[l]}
        mk_p, mv_p = mem_kv(mem_prompt, w_mk[l], w_mv[l])
        conv0 = jnp.zeros((bp, CONV_W - 1, LRU_W), x_prompt.dtype)
        h0 = jnp.zeros((bp, LRU_W), x_prompt.dtype)
        yp, (k_p, v_p, ki_p), (c_p, hl_p) = layer_forward(yp, dsa_prompt, conv0, h0, mk_p, mv_p, lw)
        attend_s = make_sample_attend(cache_k, cache_v, cache_kidx, page_table, l)
        ys, (k_s, v_s, ki_s), (c_s, hl_s) = layer_forward(ys, attend_s, state_conv[:, l], state_h[:, l],
                                                          cache_mem_k[:, l], cache_mem_v[:, l], lw)
        kp_l.append(k_p); vp_l.append(v_p); kip_l.append(ki_p); mkp_l.append(mk_p); mvp_l.append(mv_p)
        cp_l.append(c_p); hp_l.append(hl_p)
        ks_l.append(k_s); vs_l.append(v_s); kis_l.append(ki_s); cs_l.append(c_s); hs_l.append(hl_s)
    return (yp, ys, jnp.stack(kp_l, axis=1), jnp.stack(vp_l, axis=1), jnp.stack(kip_l, axis=1),
            jnp.stack(mkp_l, axis=1), jnp.stack(mvp_l, axis=1), jnp.stack(cp_l, axis=1), jnp.stack(hp_l, axis=1),
            jnp.stack(ks_l, axis=1), jnp.stack(vs_l, axis=1), jnp.stack(kis_l, axis=1),
            jnp.stack(cs_l, axis=1), jnp.stack(hs_l, axis=1))
```

```python
import functools

import jax
import jax.numpy as jnp
from jax import lax
from jax.experimental import pallas as pl
from jax.experimental.pallas import tpu as pltpu

F32 = jnp.float32
I32 = jnp.int32
MXU_DTYPE = jnp.bfloat16

TOPK_MAX = 256
MOE_TOPK = 2
LRU_C = 8.0
LN_EPS = 1e-5
Q_BLOCK = 128
KEY_CHUNK = 512
PAGES_PER_STEP = 8
VMEM_LIMIT = 56 * 1024 * 1024
NEG_INF_KEY = -0x7F800001
INT_MIN = -0x80000000


def _dot(a, b):
    return jnp.dot(a, b, preferred_element_type=F32)


def _dot_nt(a, b):
    return lax.dot_general(a, b, (((1,), (1,)), ((), ())), preferred_element_type=F32)


def _dot_hi(a, b):
    return jnp.dot(a, b, preferred_element_type=F32, precision=lax.Precision.HIGHEST)


def _mx(x):
    return x.astype(MXU_DTYPE)


def _layer_norm(x, g, b):
    mu = jnp.mean(x, axis=-1, keepdims=True)
    xc = x - mu
    var = jnp.mean(xc * xc, axis=-1, keepdims=True)
    return xc * lax.rsqrt(var + LN_EPS) * g + b


def _float_key(s):
    b = pltpu.bitcast(s, I32)
    return b ^ ((b >> 31) & 0x7FFFFFFF)


def _params(*sem):
    return pltpu.CompilerParams(dimension_semantics=sem, vmem_limit_bytes=VMEM_LIMIT)


def _full(arr):
    nd = arr.ndim
    return pl.BlockSpec(arr.shape, lambda *_: (0,) * nd)


def _in_proj_kernel(x_ref, wk_ref, wv_ref, wki_ref, wxl_ref, wgl_ref, wqt_ref, wqit_ref, wwit_ref, wvt_ref,
                    k_ref, kb_ref, v_ref, ki_ref, kib_ref, xl_ref, gl_ref, qt_ref, qit_ref, wit_ref, vt_ref,
                    *, q_scale):
    xb = _mx(x_ref[0])
    k = _dot(xb, wk_ref[...])
    k_ref[0] = k
    kb_ref[0] = _mx(k)
    v_ref[0] = _dot(xb, wv_ref[...])
    ki = _dot(xb, wki_ref[...])
    ki_ref[0] = ki
    kib_ref[0] = _mx(ki)
    xl_ref[0] = _dot(xb, wxl_ref[...])
    gl_ref[0] = _dot(xb, wgl_ref[...])
    qt_ref[0] = _mx(_dot_nt(wqt_ref[...], xb) * q_scale)
    qit_ref[0] = _mx(_dot_nt(wqit_ref[...], xb))
    wit_ref[0] = _dot_nt(wwit_ref[...], xb)
    vt_ref[0] = _mx(_dot_nt(wvt_ref[...], xb))


def _in_proj(x, w, dims, tm):
    b, s, d = x.shape
    aw, iw, idh, ih, lw = dims["att_w"], dims["idx_w"], dims["idx_dh"], dims["idx_heads"], dims["lru_w"]
    outs = [
        ((b, s, aw), F32), ((b, s, aw), MXU_DTYPE), ((b, s, aw), F32),
        ((b, s, idh), F32), ((b, s, idh), MXU_DTYPE), ((b, s, lw), F32), ((b, s, lw), F32),
        ((b, aw, s), MXU_DTYPE), ((b, iw, s), MXU_DTYPE), ((b, ih, s), F32), ((b, aw, s), MXU_DTYPE),
    ]
    nat = lambda n: pl.BlockSpec((1, tm, n), lambda bi, i: (bi, i, 0))
    tr = lambda n: pl.BlockSpec((1, n, tm), lambda bi, i: (bi, 0, i))
    out_specs = [nat(aw), nat(aw), nat(aw), nat(idh), nat(idh), nat(lw), nat(lw), tr(aw), tr(iw), tr(ih), tr(aw)]
    ws = [w["k"], w["v"], w["ki"], w["xl"], w["gl"], w["qT"], w["qiT"], w["wiT"], w["vT"]]
    return pl.pallas_call(
        functools.partial(_in_proj_kernel, q_scale=dims["att_dh"] ** -0.5),
        grid=(b, s // tm),
        in_specs=[pl.BlockSpec((1, tm, d), lambda bi, i: (bi, i, 0))] + [_full(a) for a in ws],
        out_specs=out_specs,
        out_shape=[jax.ShapeDtypeStruct(sh, dt) for sh, dt in outs],
        compiler_params=_params("parallel", "parallel"),
        name="in_proj",
    )(x, *ws)


def _dsa_prompt_kernel(qit_ref, wit_ref, qt_ref, kidx_ref, k_ref, vt_ref, att_ref,
                       keys_s, bias_s, m_s, l_s, acc_s, *, n_top, kc, n_heads, dh, idx_heads, idx_dh, idx_scale):
    qb = Q_BLOCK
    j = pl.program_id(1)
    n_chunks = (j * qb) // kc + 1
    q_pos = j * qb + lax.broadcasted_iota(I32, (1, qb), 1)

    wi = wit_ref[0] * idx_scale
    qi_all = jnp.concatenate([qit_ref[0, h * idx_dh:(h + 1) * idx_dh, :] for h in range(idx_heads)], axis=1)

    def score_chunk(c, carry):
        r0 = pl.multiple_of(c * kc, kc)
        d = _dot(kidx_ref[0, pl.ds(r0, kc), :], qi_all)
        s = jnp.zeros((kc, qb), F32)
        for h in range(idx_heads):
            s = s + jnp.maximum(d[:, h * qb:(h + 1) * qb], 0.0) * wi[h:h + 1, :]
        row = r0 + lax.broadcasted_iota(I32, (kc, qb), 0)
        s = jnp.where(row <= q_pos, s, -jnp.inf)
        keys_s[pl.ds(r0, kc), :] = _float_key(s)
        return carry

    lax.fori_loop(0, n_chunks, score_chunk, 0)

    def count(pred):
        def body(c, acc):
            r0 = pl.multiple_of(c * kc, kc)
            hit = pred(keys_s[pl.ds(r0, kc), :]).astype(I32)
            return acc + jnp.sum(hit.reshape(kc // 8, 8, qb), axis=0)
        acc = lax.fori_loop(0, n_chunks, body, jnp.zeros((8, qb), I32))
        return jnp.sum(acc, axis=0, keepdims=True)

    def bisect(i, res):
        cand = res ^ (jnp.int32(1) << (31 - i))
        cnt = count(lambda kk: kk >= cand)
        return jnp.where(cnt >= n_top, cand, res)

    thr = lax.fori_loop(0, 32, bisect, jnp.full((1, qb), INT_MIN, I32))
    thr = jnp.maximum(thr, NEG_INF_KEY + 1)
    n_gt = count(lambda kk: kk > thr)
    n_ge = count(lambda kk: kk >= thr)
    has_ties = jnp.max(n_ge) > n_top

    @pl.when(jnp.logical_not(has_ties))
    def _():
        def body(c, carry):
            r0 = pl.multiple_of(c * kc, kc)
            bias_s[pl.ds(r0, kc), :] = jnp.where(keys_s[pl.ds(r0, kc), :] >= thr, 0.0, -jnp.inf)
            return carry
        lax.fori_loop(0, n_chunks, body, 0)

    @pl.when(has_ties)
    def _():
        need = (n_top - n_gt).astype(F32)
        ri = lax.broadcasted_iota(I32, (kc, kc), 0)
        ci = lax.broadcasted_iota(I32, (kc, kc), 1)
        lower = _mx((ci < ri).astype(F32))

        def body(c, seen):
            r0 = pl.multiple_of(c * kc, kc)
            kk = keys_s[pl.ds(r0, kc), :]
            eq = kk == thr
            before = seen + _dot(lower, _mx(eq.astype(F32)))
            sel = (kk > thr) | (eq & (before < need))
            bias_s[pl.ds(r0, kc), :] = jnp.where(sel, 0.0, -jnp.inf)
            return seen + jnp.sum(eq.astype(F32), axis=0, keepdims=True)
        lax.fori_loop(0, n_chunks, body, jnp.zeros((1, qb), F32))

    n_pairs = n_heads // 2
    m_s[...] = jnp.full(m_s.shape, -jnp.inf, F32)
    l_s[...] = jnp.zeros(l_s.shape, F32)
    acc_s[...] = jnp.zeros(acc_s.shape, F32)
    top_half = lax.broadcasted_iota(I32, (2 * dh, qb), 0) < dh
    qbd = []
    for p in range(n_pairs):
        qp = qt_ref[0, 2 * dh * p:2 * dh * (p + 1), :]
        zero = jnp.zeros_like(qp)
        qbd.append(jnp.concatenate([jnp.where(top_half, qp, zero), jnp.where(top_half, zero, qp)], axis=1))

    def attend_chunk(c, carry):
        r0 = pl.multiple_of(c * kc, kc)
        bias = bias_s[pl.ds(r0, kc), :]
        bias2 = jnp.concatenate([bias, bias], axis=1)
        for p in range(n_pairs):
            kp = k_ref[0, pl.ds(r0, kc), 2 * dh * p:2 * dh * (p + 1)]
            lg = _dot(kp, qbd[p]) + bias2
            m_old = m_s[p:p + 1, :]
            m_new = jnp.maximum(m_old, jnp.max(lg, axis=0, keepdims=True))
            m_safe = jnp.where(m_new == -jnp.inf, 0.0, m_new)
            alpha = jnp.exp(m_old - m_safe)
            pr = jnp.exp(lg - m_safe)
            l_s[p:p + 1, :] = alpha * l_s[p:p + 1, :] + jnp.sum(pr, axis=0, keepdims=True)
            m_s[p:p + 1, :] = m_new
            prb = _mx(pr)
            for hh in range(2):
                h = 2 * p + hh
                pv = _dot(vt_ref[0, h * dh:(h + 1) * dh, pl.ds(r0, kc)], prb[:, hh * qb:(hh + 1) * qb])
                acc_s[h * dh:(h + 1) * dh, :] = acc_s[h * dh:(h + 1) * dh, :] * alpha[:, hh * qb:(hh + 1) * qb] + pv
        return carry

    lax.fori_loop(0, n_chunks, attend_chunk, 0)

    outs = []
    for h in range(n_heads):
        p, hh = divmod(h, 2)
        outs.append(acc_s[h * dh:(h + 1) * dh, :] / l_s[p:p + 1, hh * qb:(hh + 1) * qb])
    att_ref[0] = jnp.concatenate(outs, axis=0).T.astype(att_ref.dtype)


def _dsa_prompt(qit, wit, qt, kib, kb, vt, dims):
    b, s, aw = kb.shape
    iw, ih, idh = dims["idx_w"], dims["idx_heads"], dims["idx_dh"]
    n_top = min(TOPK_MAX, s // 4)
    kc = min(KEY_CHUNK, s)
    assert s % kc == 0 and s % Q_BLOCK == 0 and kc % Q_BLOCK == 0
    kern = functools.partial(_dsa_prompt_kernel, n_top=n_top, kc=kc, n_heads=dims["heads"], dh=dims["att_dh"],
                             idx_heads=ih, idx_dh=idh, idx_scale=dims["idx_scale"])
    blk = lambda n: pl.BlockSpec((1, n, Q_BLOCK), lambda bi, j: (bi, 0, j))
    res = lambda shp: pl.BlockSpec((1,) + shp, lambda bi, j: (bi, 0, 0))
    return pl.pallas_call(
        kern,
        grid=(b, s // Q_BLOCK),
        in_specs=[blk(iw), blk(ih), blk(aw), res((s, idh)), res((s, aw)), res((aw, s))],
        out_specs=pl.BlockSpec((1, Q_BLOCK, aw), lambda bi, j: (bi, j, 0)),
        out_shape=jax.ShapeDtypeStruct((b, s, aw), MXU_DTYPE),
        scratch_shapes=[pltpu.VMEM((s, Q_BLOCK), I32), pltpu.VMEM((s, Q_BLOCK), F32),
                        pltpu.VMEM((dims["heads"] // 2, 2 * Q_BLOCK), F32),
                        pltpu.VMEM((dims["heads"] // 2, 2 * Q_BLOCK), F32),
                        pltpu.VMEM((aw, Q_BLOCK), F32)],
        compiler_params=_params("parallel", "arbitrary"),
        name="dsa_prompt",
    )(qit, wit, qt, kib, kb, vt)


def _sample_score_kernel(pt_ref, qi_ref, w_ref, kin_ref, *rest, pps, n_q, idx_heads):
    page_refs, (sp_ref, sn_ref) = rest[:pps], rest[pps:]
    qi = qi_ref[0]
    w = w_ref[0]

    def score(kx):
        d = _dot_nt(qi, _mx(kx))
        wd = jnp.maximum(d, 0.0) * w
        return jnp.zeros((n_q, kx.shape[0]), F32) + jnp.sum(wd.reshape(n_q, idx_heads, kx.shape[0]), axis=1)

    for r in range(pps):
        n = page_refs[r].shape[0]
        sp_ref[0, :, r * n:(r + 1) * n] = score(page_refs[r][...])

    @pl.when(pl.program_id(1) == 0)
    def _():
        sn_ref[0] = score(kin_ref[0])


def _sample_scores(page_table, qi_rows, w_rows, ki_new_pad, cache_kidx, layer, n_q, idx_heads, pps):
    db, n_pages = page_table.shape
    page, idh = cache_kidx.shape[2], cache_kidx.shape[3]
    rows = qi_rows.shape[1]
    npad = ki_new_pad.shape[1]

    def page_spec(r):
        return pl.BlockSpec((None, None, page, idh), lambda b, i, pt: (pt[b, i * pps + r], layer, 0, 0))

    grid_spec = pltpu.PrefetchScalarGridSpec(
        num_scalar_prefetch=1,
        grid=(db, n_pages // pps),
        in_specs=[pl.BlockSpec((1, rows, idh), lambda b, i, pt: (b, 0, 0)),
                  pl.BlockSpec((1, rows, 1), lambda b, i, pt: (b, 0, 0)),
                  pl.BlockSpec((1, npad, idh), lambda b, i, pt: (b, 0, 0))] + [page_spec(r) for r in range(pps)],
        out_specs=[pl.BlockSpec((1, n_q, pps * page), lambda b, i, pt: (b, 0, i)),
                   pl.BlockSpec((1, n_q, npad), lambda b, i, pt: (b, 0, 0))],
    )
    return pl.pallas_call(
        functools.partial(_sample_score_kernel, pps=pps, n_q=n_q, idx_heads=idx_heads),
        grid_spec=grid_spec,
        out_shape=[jax.ShapeDtypeStruct((db, n_q, n_pages * page), F32), jax.ShapeDtypeStruct((db, n_q, npad), F32)],
        compiler_params=_params("parallel", "arbitrary"),
        name="sample_scores",
    )(page_table, qi_rows, w_rows, ki_new_pad, *([cache_kidx] * pps))


def _sample_select_kernel(sp_ref, sn_ref, bp_ref, bn_ref, kp_s, kn_s, *, n_top, n_q, lc):
    rows, past = sp_ref.shape
    npad = sn_ref.shape[1]
    n_lc = past // lc
    t_idx = lax.broadcasted_iota(I32, (rows, npad), 0) % n_q
    c_idx = lax.broadcasted_iota(I32, (rows, npad), 1)
    kn_s[...] = _float_key(jnp.where(c_idx <= t_idx, sn_ref[...], -jnp.inf))
    for c in range(n_lc):
        kp_s[:, c * lc:(c + 1) * lc] = _float_key(sp_ref[:, c * lc:(c + 1) * lc])

    def count(pred):
        acc = pred(kn_s[...]).astype(I32)
        for c in range(n_lc):
            hit = pred(kp_s[:, c * lc:(c + 1) * lc]).astype(I32)
            for q in range(lc // npad):
                acc = acc + hit[:, q * npad:(q + 1) * npad]
        return jnp.sum(acc, axis=1, keepdims=True)

    def bisect(i, res):
        cand = res ^ (jnp.int32(1) << (31 - i))
        cnt = count(lambda kk: kk >= cand)
        return jnp.where(cnt >= n_top, cand, res)

    thr = lax.fori_loop(0, 32, bisect, jnp.full((rows, 1), INT_MIN, I32))
    thr = jnp.maximum(thr, NEG_INF_KEY + 1)
    need = (n_top - count(lambda kk: kk > thr)).astype(F32)

    ri = lax.broadcasted_iota(I32, (npad, npad), 0)
    ci = lax.broadcasted_iota(I32, (npad, npad), 1)
    upper = _mx((ri < ci).astype(F32))

    def select(kk, seen):
        eq = kk == thr
        before = seen + _dot(_mx(eq.astype(F32)), upper)
        sel = (kk > thr) | (eq & (before < need))
        return jnp.where(sel, 0.0, -jnp.inf), seen + jnp.sum(eq.astype(F32), axis=1, keepdims=True)

    def body(c, seen):
        c0 = pl.multiple_of(c * npad, npad)
        bias, seen = select(kp_s[:, pl.ds(c0, npad)], seen)
        bp_ref[:, pl.ds(c0, npad)] = bias
        return seen

    seen = lax.fori_loop(0, past // npad, body, jnp.zeros((rows, 1), F32))
    bn_ref[...] = select(kn_s[...], seen)[0]


def _sample_select(scores_past, scores_new, n_top, n_q):
    rows, past = scores_past.shape
    npad = scores_new.shape[1]
    lc = min(2048, past)
    assert past % lc == 0 and lc % npad == 0
    return pl.pallas_call(
        functools.partial(_sample_select_kernel, n_top=n_top, n_q=n_q, lc=lc),
        out_shape=[jax.ShapeDtypeStruct((rows, past), F32), jax.ShapeDtypeStruct((rows, npad), F32)],
        scratch_shapes=[pltpu.VMEM((rows, past), I32), pltpu.VMEM((rows, npad), I32)],
        compiler_params=pltpu.CompilerParams(vmem_limit_bytes=VMEM_LIMIT),
        name="sample_select",
    )(scores_past, scores_new)


def _sample_attend_kernel(pt_ref, q_ref, bp_ref, bn_ref, kn_ref, vn_ref, *rest, pps, n_q, n_heads, dh):
    k_refs, v_refs = rest[:pps], rest[pps:2 * pps]
    o_ref, m_s, l_s, acc_s = rest[2 * pps:]
    i = pl.program_id(1)
    q = q_ref[0]
    rows = q.shape[0]

    def rep(bias):
        n = bias.shape[1]
        return jnp.broadcast_to(bias[:, None, :], (n_q, n_heads, n)).reshape(rows, n)

    def update(lg, vals):
        m_old = m_s[...]
        m_new = m_old
        for x in lg:
            m_new = jnp.maximum(m_new, jnp.max(x, axis=1, keepdims=True))
        m_safe = jnp.where(m_new == -jnp.inf, 0.0, m_new)
        alpha = jnp.exp(m_old - m_safe)
        l_new = alpha * l_s[...]
        acc = alpha * acc_s[...]
        for x, v in zip(lg, vals):
            pr = jnp.exp(x - m_safe)
            l_new = l_new + jnp.sum(pr, axis=1, keepdims=True)
            acc = acc + _dot(_mx(pr), _mx(v))
        m_s[...] = m_new
        l_s[...] = l_new
        acc_s[...] = acc

    @pl.when(i == 0)
    def _():
        m_s[...] = jnp.full(m_s.shape, -jnp.inf, F32)
        l_s[...] = jnp.zeros(l_s.shape, F32)
        acc_s[...] = jnp.zeros(acc_s.shape, F32)
        update([_dot_nt(q, _mx(kn_ref[0])) + rep(bn_ref[0])], [vn_ref[0]])

    page = k_refs[0].shape[0]
    lg = [_dot_nt(q, _mx(k_refs[r][...])) + rep(bp_ref[0, :, r * page:(r + 1) * page]) for r in range(pps)]
    update(lg, [v_refs[r][...] for r in range(pps)])

    @pl.when(i == pl.num_programs(1) - 1)
    def _():
        o = acc_s[...] / l_s[...]
        hr = lax.broadcasted_iota(I32, o.shape, 0) % n_heads
        hc = lax.broadcasted_iota(I32, o.shape, 1) // dh
        o = jnp.where(hr == hc, o, 0.0)
        o_ref[0] = jnp.sum(o.reshape(n_q, n_heads, n_heads * dh), axis=1)


def _sample_attend(page_table, q_bd, bias_past, bias_new, k_new_pad, v_new_pad, cache_k4, cache_v4, layer,
                   n_q, n_heads, dh, pps):
    db, n_pages = page_table.shape
    page, aw = cache_k4.shape[2], cache_k4.shape[3]
    rows = q_bd.shape[1]
    npad = k_new_pad.shape[1]

    def page_spec(r):
        return pl.BlockSpec((None, None, page, aw), lambda b, i, pt: (pt[b, i * pps + r], layer, 0, 0))

    per_b = lambda shp: pl.BlockSpec((1,) + shp, lambda b, i, pt: (b, 0, 0))
    grid_spec = pltpu.PrefetchScalarGridSpec(
        num_scalar_prefetch=1,
        grid=(db, n_pages // pps),
        in_specs=[per_b((rows, aw)),
                  pl.BlockSpec((1, n_q, pps * page), lambda b, i, pt: (b, 0, i)),
                  per_b((n_q, npad)), per_b((npad, aw)), per_b((npad, aw))]
                 + [page_spec(r) for r in range(pps)] * 2,
        out_specs=per_b((n_q, aw)),
        scratch_shapes=[pltpu.VMEM((rows, 1), F32), pltpu.VMEM((rows, 1), F32), pltpu.VMEM((rows, aw), F32)],
    )
    return pl.pallas_call(
        functools.partial(_sample_attend_kernel, pps=pps, n_q=n_q, n_heads=n_heads, dh=dh),
        grid_spec=grid_spec,
        out_shape=jax.ShapeDtypeStruct((db, n_q, aw), F32),
        compiler_params=_params("parallel", "arbitrary"),
        name="sample_attend",
    )(page_table, q_bd, bias_past, bias_new, k_new_pad, v_new_pad, *([cache_k4] * pps), *([cache_v4] * pps))


def _gelu_tanh(x):
    return 0.5 * x * (1.0 + jnp.tanh(0.7978845608028654 * (x + 0.044715 * (x * x * x))))


def _lru_gates(xc, wa, ba, wx, bx, lam):
    xb = _mx(xc)
    r = jax.nn.sigmoid(_dot(xb, wa) + ba)
    ig = jax.nn.sigmoid(_dot(xb, wx) + bx)
    nl = -lam
    softplus = jnp.maximum(nl, 0.0) + jnp.log1p(jnp.exp(-jnp.abs(nl)))
    log_a = (-LRU_C * softplus) * r
    a = jnp.exp(log_a)
    u = jnp.sqrt(-jnp.tanh(log_a) * (a * a + 1.0)) * (ig * xc)
    return a, u


def _rglru_seq_kernel(xl_ref, gl_ref, c0_ref, h0_ref, cw_ref, cb_ref, wa_ref, ba_ref, wx_ref, bx_ref, lam_ref,
                      rec_ref, hl_ref, xbuf, hc, *, conv_w):
    i = pl.program_id(1)
    tm = xl_ref.shape[1]
    npre = conv_w - 1

    @pl.when(i == 0)
    def _():
        xbuf[8 - npre:8, :] = c0_ref[0]
        hc[...] = h0_ref[0]

    x = xl_ref[0]
    xbuf[8:8 + tm, :] = x
    xc = cb_ref[...] + cw_ref[npre:npre + 1, :] * x
    for jj in range(npre):
        xc = xc + cw_ref[jj:jj + 1, :] * xbuf[8 - npre + jj:8 - npre + jj + tm, :]
    xbuf[0:8, :] = xbuf[tm:tm + 8, :]

    a, u = _lru_gates(xc, wa_ref[...], ba_ref[...], wx_ref[...], bx_ref[...], lam_ref[...])
    row = lax.broadcasted_iota(I32, a.shape, 0)
    dlt = 1
    while dlt < tm:
        a_sh = jnp.where(row >= dlt, pltpu.roll(a, dlt, 0), 1.0)
        u_sh = jnp.where(row >= dlt, pltpu.roll(u, dlt, 0), 0.0)
        u = a * u_sh + u
        a = a * a_sh
        dlt *= 2
    hs = a * hc[...] + u
    hc[...] = hs[tm - 1:tm, :]
    hl_ref[0] = hs[tm - 1:tm, :]
    rec_ref[0] = (hs * _gelu_tanh(gl_ref[0])).astype(rec_ref.dtype)


def _rglru_seq(xl, gl, conv0, h0, w, tm):
    b, s, lw = xl.shape
    conv_w = w["conv_w"].shape[0]
    assert s % tm == 0 and tm >= 8
    tile = pl.BlockSpec((1, tm, lw), lambda bi, i: (bi, i, 0))
    per_b = lambda n: pl.BlockSpec((1, n, lw), lambda bi, i: (bi, 0, 0))
    ws = [w["conv_w"], w["conv_b"], w["wa"], w["ba"], w["wx"], w["bx"], w["lam"]]
    return pl.pallas_call(
        functools.partial(_rglru_seq_kernel, conv_w=conv_w),
        grid=(b, s // tm),
        in_specs=[tile, tile, per_b(conv_w - 1), per_b(1)] + [_full(a) for a in ws],
        out_specs=[tile, per_b(1)],
        out_shape=[jax.ShapeDtypeStruct((b, s, lw), MXU_DTYPE), jax.ShapeDtypeStruct((b, 1, lw), F32)],
        scratch_shapes=[pltpu.VMEM((tm + 8, lw), F32), pltpu.VMEM((1, lw), F32)],
        compiler_params=_params("parallel", "arbitrary"),
        name="rglru_seq",
    )(xl, gl, conv0, h0, *ws)


def _rglru_step_kernel(xl_ref, gl_ref, c0_ref, h0_ref, cw_ref, cb_ref, wa_ref, ba_ref, wx_ref, bx_ref, lam_ref,
                       rec_ref, hl_ref, *, conv_w):
    t_len, nb, lw = xl_ref.shape
    npre = conv_w - 1
    xp = [c0_ref[jj] for jj in range(npre)] + [xl_ref[t] for t in range(t_len)]
    xcs = []
    for t in range(t_len):
        xc = cb_ref[...] + cw_ref[npre:npre + 1, :] * xp[npre + t]
        for jj in range(npre):
            xc = xc + cw_ref[jj:jj + 1, :] * xp[jj + t]
        xcs.append(xc)
    a, u = _lru_gates(jnp.concatenate(xcs, axis=0), wa_ref[...], ba_ref[...], wx_ref[...], bx_ref[...], lam_ref[...])
    h = h0_ref[...]
    for t in range(t_len):
        h = a[t * nb:(t + 1) * nb, :] * h + u[t * nb:(t + 1) * nb, :]
        rec_ref[t] = (h * _gelu_tanh(gl_ref[t])).astype(rec_ref.dtype)
    hl_ref[...] = h


def _rglru_step(xl_tm, gl_tm, conv_tm, h0, w):
    t_len, nb, lw = xl_tm.shape
    ws = [w["conv_w"], w["conv_b"], w["wa"], w["ba"], w["wx"], w["bx"], w["lam"]]
    return pl.pallas_call(
        functools.partial(_rglru_step_kernel, conv_w=w["conv_w"].shape[0]),
        out_shape=[jax.ShapeDtypeStruct((t_len, nb, lw), MXU_DTYPE), jax.ShapeDtypeStruct((nb, lw), F32)],
        compiler_params=pltpu.CompilerParams(vmem_limit_bytes=VMEM_LIMIT),
        name="rglru_step",
    )(xl_tm, gl_tm, conv_tm, h0, *ws)


def _mem_kv_kernel(mem_ref, wk_ref, wv_ref, mk_ref, mv_ref):
    mb = _mx(mem_ref[0])
    mk_ref[0] = _dot(mb, wk_ref[...])
    mv_ref[0] = _dot(mb, wv_ref[...])


def _mem_kv(mem, wk, wv):
    b, n, d = mem.shape
    blk = pl.BlockSpec((1, n, d), lambda bi: (bi, 0, 0))
    return pl.pallas_call(
        _mem_kv_kernel,
        grid=(b,),
        in_specs=[blk, _full(wk), _full(wv)],
        out_specs=[blk, blk],
        out_shape=[jax.ShapeDtypeStruct((b, n, d), F32)] * 2,
        compiler_params=_params("parallel"),
        name="mem_kv",
    )(mem, wk, wv)


def _route(logits, n_groups, epg):
    n_exp = n_groups * epg
    lane = lax.broadcasted_iota(I32, logits.shape, 1).astype(F32)
    big = 1e9
    is_g = (lane >= n_exp) & (lane < n_exp + n_groups)
    gl = jnp.where(is_g, logits, -jnp.inf)
    g_max = jnp.max(gl, axis=1, keepdims=True)
    g_top = jnp.min(jnp.where(gl == g_max, lane, big), axis=1, keepdims=True) - n_exp
    p_group = 1.0 / jnp.sum(jnp.where(is_g, jnp.exp(gl - g_max), 0.0), axis=1, keepdims=True)
    in_grp = (lane >= g_top * epg) & (lane < (g_top + 1.0) * epg)
    el = jnp.where(in_grp, logits, -jnp.inf)
    e1 = jnp.max(el, axis=1, keepdims=True)
    i1 = jnp.min(jnp.where(el == e1, lane, big), axis=1, keepdims=True)
    el2 = jnp.where(lane == i1, -jnp.inf, el)
    e2 = jnp.max(el2, axis=1, keepdims=True)
    i2 = jnp.min(jnp.where(el2 == e2, lane, big), axis=1, keepdims=True)
    x2 = jnp.exp(e2 - e1)
    den = 1.0 + x2
    return jnp.where(lane == i1, (1.0 / den) * p_group, 0.0) + jnp.where(lane == i2, (x2 / den) * p_group, 0.0)


def _post_kernel(x_ref, att_ref, rec_ref, mk_ref, mv_ref, woa_ref, wor_ref, g1_ref, b1_ref, wmq_ref, wmo_ref,
                 g2_ref, b2_ref, wr_ref, br_ref, x2_ref, comb_ref, *, alpha, mem_heads, n_groups, epg):
    x = x_ref[0]
    mix = _dot(_mx(att_ref[0]), woa_ref[...]) + _dot(_mx(rec_ref[0]), wor_ref[...])
    x1 = _layer_norm(alpha * x + mix, g1_ref[...], b1_ref[...])
    d = x.shape[1]
    mdh = d // mem_heads
    q = _dot(_mx(x1), wmq_ref[...]) * mdh ** -0.5
    mk = _mx(mk_ref[0])
    mv = _mx(mv_ref[0])
    outs = []
    for h in range(mem_heads):
        sl = slice(h * mdh, (h + 1) * mdh)
        lg = _dot_nt(_mx(q[:, sl]), mk[:, sl])
        lg = lg - jnp.max(lg, axis=1, keepdims=True)
        pr = jnp.exp(lg)
        pr = pr / jnp.sum(pr, axis=1, keepdims=True)
        outs.append(_dot(_mx(pr), mv[:, sl]))
    o = jnp.concatenate(outs, axis=1)
    x2 = _layer_norm(alpha * x1 + _dot(_mx(o), wmo_ref[...]), g2_ref[...], b2_ref[...])
    x2_ref[0] = x2
    comb_ref[0] = _route(_dot_hi(x2, wr_ref[...]) + br_ref[...], n_groups, epg)


def _post(x, att, rec, mk, mv, w, dims, tm):
    b, s, d = x.shape
    aw, lw = att.shape[2], rec.shape[2]
    n_mem = mk.shape[1]
    tile = lambda n: pl.BlockSpec((1, tm, n), lambda bi, i: (bi, i, 0))
    per_b = pl.BlockSpec((1, n_mem, d), lambda bi, i: (bi, 0, 0))
    ws = [w["wo_att"], w["wo_rec"], w["ln1_g"], w["ln1_b"], w["w_mq"], w["w_mo"], w["ln2_g"], w["ln2_b"],
          w["w_route"], w["b_route"]]
    kern = functools.partial(_post_kernel, alpha=dims["alpha"], mem_heads=dims["mem_heads"],
                             n_groups=dims["n_groups"], epg=dims["epg"])
    return pl.pallas_call(
        kern,
        grid=(b, s // tm),
        in_specs=[tile(d), tile(aw), tile(lw), per_b, per_b] + [_full(a) for a in ws],
        out_specs=[tile(d), tile(128)],
        out_shape=[jax.ShapeDtypeStruct((b, s, d), F32), jax.ShapeDtypeStruct((b, s, 128), F32)],
        compiler_params=_params("parallel", "parallel"),
        name="post_mix",
    )(x, att, rec, mk, mv, *ws)


def _moe_kernel(x_ref, comb_ref, wg_ref, wu_ref, wd_ref, g3_ref, b3_ref, y_ref, acc_s, *, alpha, epg):
    g = pl.program_id(1)

    @pl.when(g == 0)
    def _():
        acc_s[...] = jnp.zeros(acc_s.shape, F32)

    x = x_ref[...]
    xb = _mx(x)
    comb = comb_ref[...]
    lane = lax.broadcasted_iota(I32, comb.shape, 1)
    ff = wg_ref.shape[2] // epg
    for e in range(epg):
        sl = slice(e * ff, (e + 1) * ff)
        hg = _dot(xb, wg_ref[0, :, sl])
        hu = _dot(xb, wu_ref[0, :, sl])
        ce = jnp.sum(jnp.where(lane == g * epg + e, comb, 0.0), axis=1, keepdims=True)
        act = (hg * jax.nn.sigmoid(hg)) * hu * ce
        acc_s[...] += _dot(_mx(act), wd_ref[0, sl, :])

    @pl.when(g == pl.num_programs(1) - 1)
    def _():
        y_ref[...] = _layer_norm(alpha * x + acc_s[...], g3_ref[...], b3_ref[...])


def _moe(x2, comb, w, dims, tm):
    t, d = x2.shape
    n_groups, epg = dims["n_groups"], dims["epg"]
    gf = w["w_gate"].shape[2]
    return pl.pallas_call(
        functools.partial(_moe_kernel, alpha=dims["alpha"], epg=epg),
        grid=(t // tm, n_groups),
        in_specs=[pl.BlockSpec((tm, d), lambda i, g: (i, 0)), pl.BlockSpec((tm, 128), lambda i, g: (i, 0)),
                  pl.BlockSpec((1, d, gf), lambda i, g: (g, 0, 0)), pl.BlockSpec((1, d, gf), lambda i, g: (g, 0, 0)),
                  pl.BlockSpec((1, gf, d), lambda i, g: (g, 0, 0)), _full(w["ln3_g"]), _full(w["ln3_b"])],
        out_specs=pl.BlockSpec((tm, d), lambda i, g: (i, 0)),
        out_shape=jax.ShapeDtypeStruct((t, d), F32),
        scratch_shapes=[pltpu.VMEM((tm, d), F32)],
        compiler_params=_params("parallel", "arbitrary"),
        name="hmoe",
    )(x2, comb, w["w_gate"], w["w_up"], w["w_down"], w["ln3_g"], w["ln3_b"])


def _block_diag(wb):
    n, bw, _ = wb.shape
    eye = jnp.eye(n, dtype=wb.dtype)
    return (eye[:, None, :, None] * wb[:, :, None, :]).reshape(n * bw, n * bw)


def _layer_weights(l, dims, w_in, conv_w, conv_b, w_rg_a, b_rg_a, w_rg_x, b_rg_x, lru_lambda, w_out, ln1_g, ln1_b,
                   w_mq, w_mk, w_mv, w_mo, ln2_g, ln2_b, w_router_g, b_router_g, w_router_e, b_router_e,
                   w_exp_gate, w_exp_up, w_exp_down, ln3_g, ln3_b):
    aw, iw, idh, ih, lw = dims["att_w"], dims["idx_w"], dims["idx_dh"], dims["idx_heads"], dims["lru_w"]
    n_groups, epg = dims["n_groups"], dims["epg"]
    d = w_in.shape[1]
    offs = [0]
    for n in (aw, aw, aw, iw, idh, ih, lw, lw):
        offs.append(offs[-1] + n)
    wi = w_in[l]
    col = lambda i: wi[:, offs[i]:offs[i + 1]]
    row = lambda v: v[l].reshape(1, -1).astype(F32)
    n_exp = n_groups * epg
    ff = w_exp_gate.shape[3]
    w_route = jnp.zeros((d, 128), F32)
    w_route = w_route.at[:, :n_exp].set(jnp.transpose(w_router_e[l], (1, 0, 2)).reshape(d, n_exp))
    w_route = w_route.at[:, n_exp:n_exp + n_groups].set(w_router_g[l])
    b_route = jnp.zeros((1, 128), F32)
    b_route = b_route.at[0, :n_exp].set(b_router_e[l].reshape(n_exp))
    b_route = b_route.at[0, n_exp:n_exp + n_groups].set(b_router_g[l])
    grp_cols = lambda we: _mx(jnp.transpose(we[l].reshape(n_groups, epg, d, ff), (0, 2, 1, 3)).reshape(n_groups, d, epg * ff))
    return {
        "k": _mx(col(1)), "v": _mx(col(2)), "ki": _mx(col(4)), "xl": _mx(col(6)), "gl": _mx(col(7)),
        "qT": _mx(col(0).T), "qiT": _mx(col(3).T), "wiT": _mx(col(5).T), "vT": _mx(col(2).T),
        "conv_w": conv_w[l].astype(F32), "conv_b": row(conv_b),
        "wa": _mx(_block_diag(w_rg_a[l])), "ba": row(b_rg_a), "wx": _mx(_block_diag(w_rg_x[l])), "bx": row(b_rg_x),
        "lam": row(lru_lambda),
        "wo_att": _mx(w_out[l][:aw]), "wo_rec": _mx(w_out[l][aw:]), "ln1_g": row(ln1_g), "ln1_b": row(ln1_b),
        "w_mq": _mx(w_mq[l]), "w_mk": _mx(w_mk[l]), "w_mv": _mx(w_mv[l]), "w_mo": _mx(w_mo[l]),
        "ln2_g": row(ln2_g), "ln2_b": row(ln2_b), "w_route": w_route, "b_route": b_route,
        "w_gate": grp_cols(w_exp_gate), "w_up": grp_cols(w_exp_up),
        "w_down": _mx(w_exp_down[l].reshape(n_groups, epg * ff, d)),
        "ln3_g": row(ln3_g), "ln3_b": row(ln3_b),
    }


def _tile(n, pref):
    t = min(pref, n)
    while n % t:
        t //= 2
    return t


def _tail_rows(state, xl, n):
    return jnp.concatenate([state.astype(xl.dtype), xl], axis=1)[:, -n:]


def _prompt_layer(x, mem, w, dims):
    b, s, d = x.shape
    lw = dims["lru_w"]
    k, kb, v, ki, kib, xl, gl, qt, qit, wit, vt = _in_proj(x, w, dims, _tile(s, 512))
    att = _dsa_prompt(qit, wit, qt, kib, kb, vt, dims)
    conv0 = jnp.zeros((b, dims["conv_w"] - 1, lw), F32)
    rec, h_last = _rglru_seq(xl, gl, conv0, jnp.zeros((b, 1, lw), F32), w, _tile(s, 256))
    mk, mv = _mem_kv(mem, w["w_mk"], w["w_mv"])
    x2, comb = _post(x, att, rec, mk, mv, w, dims, _tile(s, 512))
    y = _moe(x2.reshape(b * s, d), comb.reshape(b * s, 128), w, dims, _tile(b * s, 512)).reshape(b, s, d)
    return y, k, v, ki, mk, mv, _tail_rows(conv0, xl, dims["conv_w"] - 1), h_last[:, 0]


def _sample_layer(x, cache_k, cache_v, cache_kidx, mem_k, mem_v, state_conv, state_h, page_table, layer, w, dims):
    db, t_len, d = x.shape
    nh, dh, ih, idh, aw, lw = dims["heads"], dims["att_dh"], dims["idx_heads"], dims["idx_dh"], dims["att_w"], dims["lru_w"]
    nt = db * t_len
    k, _, v, ki, _, xl, gl, qt, qit, wit, _ = _in_proj(x.reshape(1, nt, d), w, dims, _tile(nt, 512))
    n_pages = page_table.shape[1]
    page = cache_k.shape[2]
    past = n_pages * page
    n_top = min(TOPK_MAX, (past + t_len) // 4)
    pps = _tile(n_pages, PAGES_PER_STEP)
    npad = 128
    pad_keys = lambda a: jnp.pad(a.reshape(db, t_len, -1), ((0, 0), (0, npad - t_len), (0, 0)))
    qi_rows = qit[0].reshape(ih, idh, db, t_len).transpose(2, 3, 0, 1).reshape(db, t_len * ih, idh)
    w_rows = (wit[0] * dims["idx_scale"]).reshape(ih, db, t_len).transpose(1, 2, 0).reshape(db, t_len * ih, 1)
    sc_past, sc_new = _sample_scores(page_table, qi_rows, w_rows, pad_keys(ki[0]), cache_kidx, layer, t_len, ih, pps)
    bias_past, bias_new = _sample_select(sc_past.reshape(nt, past), sc_new.reshape(nt, npad), n_top, t_len)
    q_rows = qt[0].reshape(nh, dh, db, t_len).transpose(2, 3, 0, 1)
    q_bd = (q_rows[:, :, :, None, :] * jnp.eye(nh, dtype=q_rows.dtype)[None, None, :, :, None]).reshape(db, t_len * nh, aw)
    cache_k4 = cache_k.reshape(cache_k.shape[:3] + (aw,))
    cache_v4 = cache_v.reshape(cache_v.shape[:3] + (aw,))
    att = _sample_attend(page_table, q_bd, bias_past.reshape(db, t_len, past), bias_new.reshape(db, t_len, npad),
                         pad_keys(k[0]), pad_keys(v[0]), cache_k4, cache_v4, layer, t_len, nh, dh, pps)
    tm_major = lambda a: a.reshape(db, t_len, lw).transpose(1, 0, 2)
    rec_tm, h_last = _rglru_step(tm_major(xl[0]), tm_major(gl[0]), state_conv.transpose(1, 0, 2), state_h, w)
    rec = rec_tm.transpose(1, 0, 2)
    x2, comb = _post(x, att, rec, mem_k.reshape(db, -1, d), mem_v.reshape(db, -1, d), w, dims, t_len)
    y = _moe(x2.reshape(nt, d), comb.reshape(nt, 128), w, dims, _tile(nt, 512)).reshape(db, t_len, d)
    shp = lambda a, n: a[0].reshape(db, t_len, n)
    return (y, shp(k, aw), shp(v, aw), shp(ki, idh),
            _tail_rows(state_conv, shp(xl, lw), dims["conv_w"] - 1), h_last)


def kernel(x_prompt, x_sample, cache_k, cache_v, cache_kidx, cache_mem_k, cache_mem_v, state_conv, state_h, page_table, mem_prompt, w_in, conv_w, conv_b, w_rg_a, b_rg_a, w_rg_x, b_rg_x, lru_lambda, w_out, ln1_g, ln1_b, w_mq, w_mk, w_mv, w_mo, ln2_g, ln2_b, w_router_g, b_router_g, w_router_e, b_router_e, w_exp_gate, w_exp_up, w_exp_down, ln3_g, ln3_b):
    depth = w_in.shape[0]
    nh, dh = cache_k.shape[3], cache_k.shape[4]
    idh = cache_kidx.shape[3]
    lw = state_h.shape[2]
    aw = nh * dh
    ih = (w_in.shape[2] - 3 * aw - idh - 2 * lw) // (idh + 1)
    dims = {
        "heads": nh, "att_dh": dh, "att_w": aw, "idx_dh": idh, "idx_heads": ih, "idx_w": ih * idh,
        "idx_scale": float(ih * idh) ** -0.5, "lru_w": lw, "conv_w": conv_w.shape[1],
        "mem_heads": cache_mem_k.shape[3], "n_groups": w_router_e.shape[1], "epg": w_router_e.shape[3],
        "alpha": (2.0 * depth) ** 0.25,
    }
    assert dims["n_groups"] * (dims["epg"] + 1) <= 128 and nh % 2 == 0 and 2 * dh == Q_BLOCK
    bp, s = x_prompt.shape[:2]
    db, t_len = x_sample.shape[:2]
    mh, mdh = cache_mem_k.shape[3], cache_mem_k.shape[4]
    yp, ys = x_prompt, x_sample
    acc = [[] for _ in range(12)]
    for l in range(depth):
        w = _layer_weights(l, dims, w_in, conv_w, conv_b, w_rg_a, b_rg_a, w_rg_x, b_rg_x, lru_lambda, w_out, ln1_g,
                           ln1_b, w_mq, w_mk, w_mv, w_mo, ln2_g, ln2_b, w_router_g, b_router_g, w_router_e,
                           b_router_e, w_exp_gate, w_exp_up, w_exp_down, ln3_g, ln3_b)
        yp, k_p, v_p, ki_p, mk_p, mv_p, c_p, h_p = _prompt_layer(yp, mem_prompt, w, dims)
        ys, k_s, v_s, ki_s, c_s, h_s = _sample_layer(ys, cache_k, cache_v, cache_kidx, cache_mem_k[:, l],
                                                     cache_mem_v[:, l], state_conv[:, l], state_h[:, l], page_table,
                                                     l, w, dims)
        vals = (k_p.reshape(bp, s, nh, dh), v_p.reshape(bp, s, nh, dh), ki_p,
                mk_p.reshape(bp, -1, mh, mdh), mv_p.reshape(bp, -1, mh, mdh), c_p, h_p,
                k_s.reshape(db, t_len, nh, dh), v_s.reshape(db, t_len, nh, dh), ki_s, c_s, h_s)
        for lst, val in zip(acc, vals):
            lst.append(val)
    return (yp, ys) + tuple(jnp.stack(lst, axis=1) for lst in acc)
```
